```python
import math
import jax, jax.numpy as jnp
from jax import lax
import numpy as np

D_MODEL = 1024
BATCH = 2
SEQ = 8192
DEPTH = 2

GRID_W = 64
CTX_LEN = 256
HEAD_DIM = 64
N_Q_HEADS = 8
N_KV_HEADS = 2
Q_PER_KV = N_Q_HEADS // N_KV_HEADS
ATTN_WIDTH = N_Q_HEADS * HEAD_DIM
KV_WIDTH = N_KV_HEADS * HEAD_DIM
WINDOW = 128
BLOCK = 128
ROPE_BASE = 10000.0
POOL_WINDOWS = (2, 4, 8, 16)
POOL_GROUP = 64
POOL_WIDTH = POOL_GROUP * len(POOL_WINDOWS)
CONV_WIDTH = 256
CONV_K = 31
N_BRANCH = 3
SPLIT_POOL = POOL_WIDTH
SPLIT_Q = SPLIT_POOL + ATTN_WIDTH
SPLIT_K = SPLIT_Q + KV_WIDTH
SPLIT_V = SPLIT_K + KV_WIDTH
SPLIT_CONV = SPLIT_V + 2 * CONV_WIDTH
IN_WIDTH = SPLIT_CONV + N_BRANCH * D_MODEL
D_FF = 4 * D_MODEL
ALPHA = (2 * DEPTH) ** 0.25
BETA = (8 * DEPTH) ** -0.25
LN_EPS = 1e-5
NEG = -1e30

kernel_name = "hybrid_pool_swa_conformer_deepnorm_dit"


def _layer_norm(x):
    xf = x.astype(jnp.float32)
    mu = jnp.mean(xf, axis=-1, keepdims=True)
    var = jnp.mean(jnp.square(xf - mu), axis=-1, keepdims=True)
    return ((xf - mu) * lax.rsqrt(var + LN_EPS)).astype(x.dtype)


def _modulate(x, shift, scale):
    return _layer_norm(x) * (1 + scale) + shift


def _post_norm(x, gate, y, g, b):
    return _layer_norm(ALPHA * x + gate * y) * g + b


def _pool_mixer(u, pool_w, pool_scale):
    B, L, _ = u.shape
    uf = u.astype(jnp.float32)
    cs = jnp.concatenate([jnp.zeros((B, 1, POOL_WIDTH), jnp.float32), jnp.cumsum(uf, axis=1)], axis=1)
    t = jnp.arange(L)
    outs = []
    for g, w in enumerate(POOL_WINDOWS):
        lo = jnp.clip(t - w // 2, 0, L)
        hi = jnp.clip(t + w - w // 2, 0, L)
        csg = cs[..., g * POOL_GROUP:(g + 1) * POOL_GROUP]
        cnt = (hi - lo).astype(jnp.float32)[None, :, None]
        outs.append((csg[:, hi] - csg[:, lo]) / cnt)
    pooled = jnp.concatenate(outs, axis=-1)
    d = (pooled - uf).astype(u.dtype).reshape(B, L, len(POOL_WINDOWS), POOL_GROUP)
    y = jnp.einsum('blgc,gcd->blgd', d, pool_w).reshape(B, L, POOL_WIDTH)
    return y * pool_scale


def _axial_rope(x, row_pos, col_pos):
    half = HEAD_DIM // 2
    quarter = half // 2
    inv = ROPE_BASE ** (-jnp.arange(quarter, dtype=jnp.float32) / quarter)

    def rot(xa, pos):
        ang = pos.astype(jnp.float32)[:, None] * inv[None, :]
        cos = jnp.cos(ang)[None, :, None, :]
        sin = jnp.sin(ang)[None, :, None, :]
        x1, x2 = xa[..., :quarter], xa[..., quarter:]
        return jnp.concatenate([x1 * cos - x2 * sin, x1 * sin + x2 * cos], axis=-1)

    xf = x.astype(jnp.float32)
    out = jnp.concatenate([rot(xf[..., :half], row_pos), rot(xf[..., half:], col_pos)], axis=-1)
    return out.astype(x.dtype)


def _latent_attention(q, k, v, kc, vc, sink):
    B, L = q.shape[0], q.shape[1]
    C = kc.shape[1]
    nb = L // BLOCK
    scale = HEAD_DIM ** -0.5
    qb = q.reshape(B, nb, BLOCK, N_KV_HEADS, Q_PER_KV, HEAD_DIM)
    pad = jnp.zeros((B, BLOCK, N_KV_HEADS, HEAD_DIM), k.dtype)

    def band(t):
        tp = jnp.concatenate([pad, t, pad], axis=1).reshape(B, nb + 2, BLOCK, N_KV_HEADS, HEAD_DIM)
        return jnp.concatenate([tp[:, :-2], tp[:, 1:-1], tp[:, 2:]], axis=2)

    kb, vb = band(k), band(v)
    s_loc = jnp.einsum('bnqhgd,bnkhd->bnhgqk', qb, kb).astype(jnp.float32) * scale
    qi = jnp.arange(nb)[:, None, None] * BLOCK + jnp.arange(BLOCK)[None, :, None]
    kj = (jnp.arange(nb)[:, None, None] - 1) * BLOCK + jnp.arange(3 * BLOCK)[None, None, :]
    valid = (jnp.abs(qi - kj) <= WINDOW) & (kj >= 0) & (kj < L)
    s_loc = jnp.where(valid[None, :, None, None], s_loc, NEG)
    s_ctx = jnp.einsum('bnqhgd,bchd->bnhgqc', qb, kc).astype(jnp.float32) * scale
    s_sink = jnp.broadcast_to(
        sink.reshape(N_KV_HEADS, Q_PER_KV)[None, None, :, :, None, None].astype(jnp.float32),
        s_loc.shape[:-1] + (1,))
    p = jax.nn.softmax(jnp.concatenate([s_loc, s_ctx, s_sink], axis=-1), axis=-1)
    p_loc = p[..., :3 * BLOCK].astype(v.dtype)
    p_ctx = p[..., 3 * BLOCK:3 * BLOCK + C].astype(v.dtype)
    o = (jnp.einsum('bnhgqk,bnkhd->bnqhgd', p_loc, vb)
         + jnp.einsum('bnhgqc,bchd->bnqhgd', p_ctx, vc))
    return o.reshape(B, L, ATTN_WIDTH)


def _context_attention(qc, kc, vc, sink):
    B, C = qc.shape[0], qc.shape[1]
    scale = HEAD_DIM ** -0.5
    s = jnp.einsum('bqhgd,bkhd->bhgqk', qc, kc).astype(jnp.float32) * scale
    s_sink = jnp.broadcast_to(
        sink.reshape(N_KV_HEADS, Q_PER_KV)[None, :, :, None, None].astype(jnp.float32),
        s.shape[:-1] + (1,))
    p = jax.nn.softmax(jnp.concatenate([s, s_sink], axis=-1), axis=-1)
    o = jnp.einsum('bhgqk,bkhd->bqhgd', p[..., :C].astype(vc.dtype), vc)
    return o.reshape(B, C, ATTN_WIDTH)


def _conv_module(u, conv_w, conv_b, ln_g, ln_b, w_proj):
    a, b = jnp.split(u, 2, axis=-1)
    glu = a * jax.nn.sigmoid(b)
    y = lax.conv_general_dilated(glu, conv_w[:, None, :], window_strides=(1,),
                                 padding=[(CONV_K // 2, CONV_K // 2)],
                                 dimension_numbers=('NWC', 'WIO', 'NWC'),
                                 feature_group_count=CONV_WIDTH) + conv_b
    y = jax.nn.silu(_layer_norm(y) * ln_g + ln_b)
    return y @ w_proj


def _split_in(z):
    return jnp.split(z, [SPLIT_POOL, SPLIT_Q, SPLIT_K, SPLIT_V, SPLIT_CONV], axis=-1)


def _merge(gates, o_attn, u_pool, u_conv, pool_w, pool_scale, w_pool_out, w_attn_out,
           conv_w, conv_b, conv_ln_g, conv_ln_b, w_conv_out, w_out):
    y_pool = _pool_mixer(u_pool, pool_w, pool_scale) @ w_pool_out
    y_attn = o_attn @ w_attn_out
    y_conv = _conv_module(u_conv, conv_w, conv_b, conv_ln_g, conv_ln_b, w_conv_out)
    g = jax.nn.sigmoid(gates.astype(jnp.float32)).astype(gates.dtype)
    g_pool, g_attn, g_conv = jnp.split(g, N_BRANCH, axis=-1)
    return (g_pool * y_pool + g_attn * y_attn + g_conv * y_conv) @ w_out


def _mlp(h, w1, b1, w2, b2):
    return jnp.square(jax.nn.relu(h @ w1 + b1)) @ w2 + b2


def setup_inputs(seed: int = 0) -> dict:
    key = jax.random.key(seed)
    ks = jax.random.split(key, 32)
    f = jnp.float32
    D = D_MODEL

    def nrm(k, shape, s):
        return jax.random.normal(k, shape, f) * s

    return {
        "x": nrm(ks[0], (BATCH, SEQ, D), 1.0),
        "c": nrm(ks[1], (BATCH, D), 1.0),
        "ctx": nrm(ks[2], (BATCH, CTX_LEN, D), 1.0),
        "c_ctx": nrm(ks[3], (D,), 1.0),
        "w_mod": nrm(ks[4], (DEPTH, D, 6 * D), 0.5 * D ** -0.5),
        "b_mod": nrm(ks[5], (DEPTH, 6 * D), 0.02),
        "w_in": nrm(ks[6], (DEPTH, D, IN_WIDTH), D ** -0.5),
        "b_in": nrm(ks[7], (DEPTH, IN_WIDTH), 0.02),
        "pool_w": nrm(ks[8], (DEPTH, len(POOL_WINDOWS), POOL_GROUP, POOL_GROUP), POOL_GROUP ** -0.5),
        "pool_scale": 1.0 + nrm(ks[9], (DEPTH, POOL_WIDTH), 0.02),
        "w_pool_out": nrm(ks[10], (DEPTH, POOL_WIDTH, D), POOL_WIDTH ** -0.5),
        "attn_sink": nrm(ks[11], (DEPTH, N_Q_HEADS), 0.5),
        "w_attn_out": nrm(ks[12], (DEPTH, ATTN_WIDTH, D), ATTN_WIDTH ** -0.5),
        "conv_w": nrm(ks[13], (DEPTH, CONV_K, CONV_WIDTH), CONV_K ** -0.5),
        "conv_b": nrm(ks[14], (DEPTH, CONV_WIDTH), 0.02),
        "conv_ln_g": 1.0 + nrm(ks[15], (DEPTH, CONV_WIDTH), 0.02),
        "conv_ln_b": nrm(ks[16], (DEPTH, CONV_WIDTH), 0.02),
        "w_conv_out": nrm(ks[17], (DEPTH, CONV_WIDTH, D), CONV_WIDTH ** -0.5),
        "w_out": nrm(ks[18], (DEPTH, D, D), BETA * D ** -0.5),
        "ln1_g": 1.0 + nrm(ks[19], (DEPTH, D), 0.02),
        "ln1_b": nrm(ks[20], (DEPTH, D), 0.02),
        "w_mlp1": nrm(ks[21], (DEPTH, D, D_FF), D ** -0.5),
        "b_mlp1": nrm(ks[22], (DEPTH, D_FF), 0.02),
        "w_mlp2": nrm(ks[23], (DEPTH, D_FF, D), BETA * D_FF ** -0.5),
        "b_mlp2": nrm(ks[24], (DEPTH, D), 0.02),
        "ln2_g": 1.0 + nrm(ks[25], (DEPTH, D), 0.02),
        "ln2_b": nrm(ks[26], (DEPTH, D), 0.02),
    }


def reference(x, c, ctx, c_ctx, w_mod, b_mod, w_in, b_in, pool_w, pool_scale, w_pool_out,
              attn_sink, w_attn_out, conv_w, conv_b, conv_ln_g, conv_ln_b, w_conv_out, w_out,
              ln1_g, ln1_b, w_mlp1, b_mlp1, w_mlp2, b_mlp2, ln2_g, ln2_b):
    B, L, _ = x.shape
    C = ctx.shape[1]
    rows = L // GRID_W
    row_pos = jnp.repeat(jnp.arange(rows, dtype=jnp.int32), GRID_W)
    col_pos = jnp.tile(jnp.arange(GRID_W, dtype=jnp.int32), rows)
    silu_c = jax.nn.silu(c)
    silu_cc = jax.nn.silu(c_ctx)
    xl, xc = x, ctx
    for i in range(DEPTH):
        last = i == DEPTH - 1
        mod_l = (silu_c @ w_mod[i] + b_mod[i])[:, None, :]
        mod_c = silu_cc @ w_mod[i] + b_mod[i]
        sh1_l, sc1_l, g1_l, sh2_l, sc2_l, g2_l = jnp.split(mod_l, 6, axis=-1)
        sh1_c, sc1_c, g1_c, sh2_c, sc2_c, g2_c = jnp.split(mod_c, 6, axis=-1)
        branch_params = (pool_w[i], pool_scale[i], w_pool_out[i], w_attn_out[i], conv_w[i], conv_b[i],
                         conv_ln_g[i], conv_ln_b[i], w_conv_out[i], w_out[i])

        hl = _modulate(xl, sh1_l, sc1_l)
        hc = _modulate(xc, sh1_c, sc1_c)
        ul_pool, ul_q, ul_k, ul_v, ul_conv, ul_gate = _split_in(hl @ w_in[i] + b_in[i])
        uc_pool, uc_q, uc_k, uc_v, uc_conv, uc_gate = _split_in(hc @ w_in[i] + b_in[i])
        kc = uc_k.reshape(B, C, N_KV_HEADS, HEAD_DIM)
        vc = uc_v.reshape(B, C, N_KV_HEADS, HEAD_DIM)
        ql = _axial_rope(ul_q.reshape(B, L, N_Q_HEADS, HEAD_DIM), row_pos, col_pos)
        ql = ql.reshape(B, L, N_KV_HEADS, Q_PER_KV, HEAD_DIM)
        kl = _axial_rope(ul_k.reshape(B, L, N_KV_HEADS, HEAD_DIM), row_pos, col_pos)
        vl = ul_v.reshape(B, L, N_KV_HEADS, HEAD_DIM)
        ol = _latent_attention(ql, kl, vl, kc, vc, attn_sink[i])
        yl = _merge(ul_gate, ol, ul_pool, ul_conv, *branch_params)
        xl_new = _post_norm(xl, g1_l, yl, ln1_g[i], ln1_b[i])
        if not last:
            oc = _context_attention(uc_q.reshape(B, C, N_KV_HEADS, Q_PER_KV, HEAD_DIM), kc, vc, attn_sink[i])
            yc = _merge(uc_gate, oc, uc_pool, uc_conv, *branch_params)
            xc = _post_norm(xc, g1_c, yc, ln1_g[i], ln1_b[i])
        xl = xl_new

        yl = _mlp(_modulate(xl, sh2_l, sc2_l), w_mlp1[i], b_mlp1[i], w_mlp2[i], b_mlp2[i])
        xl = _post_norm(xl, g2_l, yl, ln2_g[i], ln2_b[i])
        if not last:
            yc = _mlp(_modulate(xc, sh2_c, sc2_c), w_mlp1[i], b_mlp1[i], w_mlp2[i], b_mlp2[i])
            xc = _post_norm(xc, g2_c, yc, ln2_g[i], ln2_b[i])
    return xl
```

```python
import functools
import math

import jax
import jax.numpy as jnp
import numpy as np
from jax import lax
from jax.experimental import pallas as pl
from jax.experimental.pallas import tpu as pltpu

LANES = 128
BF16_SUBLANES = 16
VMEM_LIMIT_BYTES = 56 * 1024 * 1024

GRID_W = 64
HEAD_DIM = 64
N_Q_HEADS = 8
N_KV_HEADS = 2
ATTN_WIDTH = N_Q_HEADS * HEAD_DIM
WINDOW = 128
BLOCK = 128
ROPE_BASE = 10000.0
POOL_WINDOWS = (2, 4, 8, 16)
POOL_GROUP = 64
POOL_WIDTH = POOL_GROUP * len(POOL_WINDOWS)
CONV_WIDTH = 256
CONV_K = 31
LN_EPS = 1e-5
NEG = -1e30

HALO = BF16_SUBLANES
POOL_CHUNK = 128
KV_DUP = 2 * N_KV_HEADS * HEAD_DIM

SEG_POOL = (0, POOL_WIDTH)
SEG_Q = (SEG_POOL[1], SEG_POOL[1] + ATTN_WIDTH)
SEG_K = (SEG_Q[1], SEG_Q[1] + KV_DUP)
SEG_V = (SEG_K[1], SEG_K[1] + KV_DUP)
SEG_CONV = (SEG_V[1], SEG_V[1] + 2 * CONV_WIDTH)
PROJ_WIDTH = SEG_CONV[1]


def _ln(x):
    mu = jnp.mean(x, axis=-1, keepdims=True)
    xc = x - mu
    var = jnp.mean(xc * xc, axis=-1, keepdims=True)
    return xc * lax.rsqrt(var + LN_EPS)


def _dot(a, b):
    return jnp.dot(a, b, preferred_element_type=jnp.float32)


def _const_spec(shape):
    nd = len(shape)
    return pl.BlockSpec(shape, lambda *_: (0,) * nd, pipeline_mode=pl.Buffered(1))


def _params(n_axes):
    return pltpu.CompilerParams(dimension_semantics=("arbitrary",) * n_axes,
                                vmem_limit_bytes=VMEM_LIMIT_BYTES)


def _mod_kernel(c_ref, w_ref, b_ref, o_ref):
    c = c_ref[...]
    s = (c * jax.nn.sigmoid(c)).astype(jnp.bfloat16)
    o_ref[0] = _dot(s, w_ref[0].astype(jnp.bfloat16)) + b_ref[0]


def _modulation(cvec, w_mod, b_mod):
    depth, d, n = w_mod.shape
    tn = 1536
    rows = cvec.shape[0]
    return pl.pallas_call(
        _mod_kernel,
        grid=(depth, n // tn),
        in_specs=[pl.BlockSpec((rows, d), lambda i, j: (0, 0)),
                  pl.BlockSpec((1, d, tn), lambda i, j: (i, 0, j)),
                  pl.BlockSpec((1, 1, tn), lambda i, j: (i, 0, j))],
        out_specs=pl.BlockSpec((1, rows, tn), lambda i, j: (i, 0, j)),
        out_shape=jax.ShapeDtypeStruct((depth, rows, n), jnp.float32),
        compiler_params=_params(2),
        name="modulation",
    )(cvec, w_mod, b_mod.reshape(depth, 1, n))


def _rope(xc, cos, sin_signed, first_half):
    partner = jnp.where(first_half, pltpu.roll(xc, LANES - 16, axis=1), pltpu.roll(xc, 16, axis=1))
    return xc * cos + partner * sin_signed


def _inproj_kernel(*refs, use_rope):
    if use_rope:
        x_ref, mod_ref, w_ref, b_ref, cos_ref, sin_ref, pool_ref, q_ref, k_ref, v_ref, conv_ref = refs
    else:
        x_ref, mod_ref, w_ref, b_ref, pool_ref, q_ref, k_ref, v_ref, conv_ref = refs
    x = x_ref[0]
    mod = mod_ref[0]
    h = _ln(x) * (1.0 + mod[1:2]) + mod[0:1]
    hb = h.astype(jnp.bfloat16)

    def proj(seg):
        return _dot(hb, w_ref[:, seg[0]:seg[1]]) + b_ref[:, seg[0]:seg[1]]

    pool_ref[0] = proj(SEG_POOL).astype(pool_ref.dtype)
    v_ref[0] = proj(SEG_V).astype(v_ref.dtype)
    conv_ref[0] = proj(SEG_CONV).astype(conv_ref.dtype)

    q = proj(SEG_Q)
    k = proj(SEG_K)
    if use_rope:
        cos = cos_ref[...]
        sin = sin_ref[...]
        lane = lax.broadcasted_iota(jnp.int32, (1, LANES), 1)
        first_half = (lane % 32) < 16
        q = jnp.concatenate([_rope(q[:, c * LANES:(c + 1) * LANES], cos, sin, first_half)
                             for c in range(ATTN_WIDTH // LANES)], axis=1)
        k = jnp.concatenate([_rope(k[:, c * LANES:(c + 1) * LANES], cos, sin, first_half)
                             for c in range(KV_DUP // LANES)], axis=1)
    q_ref[0] = (q * (HEAD_DIM ** -0.5)).astype(q_ref.dtype)
    k_ref[0] = k.astype(k_ref.dtype)


def _inproj(x, mod, w, b, rope, *, tm, mod_per_batch):
    nb, seq, d = x.shape
    use_rope = rope is not None
    mod_map = (lambda bi, j: (bi, 0, 0)) if mod_per_batch else (lambda bi, j: (0, 0, 0))
    in_specs = [pl.BlockSpec((1, tm, d), lambda bi, j: (bi, j, 0)),
                pl.BlockSpec((1, 6, d), mod_map),
                _const_spec((d, PROJ_WIDTH)),
                _const_spec((1, PROJ_WIDTH))]
    args = [x, mod, w, b]
    if use_rope:
        in_specs += [pl.BlockSpec((tm, LANES), lambda bi, j: (j, 0))] * 2
        args += list(rope)
    widths = (POOL_WIDTH, ATTN_WIDTH, KV_DUP, KV_DUP, 2 * CONV_WIDTH)
    return pl.pallas_call(
        functools.partial(_inproj_kernel, use_rope=use_rope),
        grid=(nb, seq // tm),
        in_specs=in_specs,
        out_specs=[pl.BlockSpec((1, tm, wd), lambda bi, j: (bi, j, 0)) for wd in widths],
        out_shape=[jax.ShapeDtypeStruct((nb, seq, wd), jnp.bfloat16) for wd in widths],
        compiler_params=_params(2),
        name="inproj_rope" if use_rope else "inproj",
    )(*args)


def _attn_kernel(*refs, local):
    if local:
        (sink_ref, q_ref, kp_ref, kc_ref, kn_ref, vp_ref, vc_ref, vn_ref, kx_ref, vx_ref, o_ref) = refs
    else:
        sink_ref, q_ref, kx_ref, vx_ref, o_ref = refs
    lane = lax.broadcasted_iota(jnp.int32, (1, LANES), 1)
    low = lane < HEAD_DIM
    if local:
        n = pl.program_id(1)
        last = pl.num_programs(1) - 1
        row = lax.broadcasted_iota(jnp.int32, (BLOCK, 3 * BLOCK), 0)
        col = lax.broadcasted_iota(jnp.int32, (BLOCK, 3 * BLOCK), 1)
        band = (col >= row) & (col <= row + 2 * WINDOW)
        band = band & ((col >= BLOCK) | (n > 0)) & ((col < 2 * BLOCK) | (n < last))
    zero = jnp.zeros((), jnp.bfloat16)
    for g in range(N_KV_HEADS):
        sl = slice(g * LANES, (g + 1) * LANES)
        if local:
            kk = jnp.concatenate([kp_ref[0, :, sl], kc_ref[0, :, sl], kn_ref[0, :, sl], kx_ref[0, :, sl]], axis=0)
            vv = jnp.concatenate([vp_ref[0, :, sl], vc_ref[0, :, sl], vn_ref[0, :, sl], vx_ref[0, :, sl]], axis=0)
        else:
            kk = kx_ref[0, :, sl]
            vv = vx_ref[0, :, sl]
        v_halves = (jnp.where(low, vv, zero), jnp.where(low, zero, vv))
        for c in range(LANES // HEAD_DIM):
            chunk = slice((2 * g + c) * LANES, (2 * g + c + 1) * LANES)
            qc = q_ref[0, :, chunk]
            acc = jnp.zeros((qc.shape[0], LANES), jnp.float32)
            for half in range(2):
                head = (2 * g + c) * 2 + half
                qm = jnp.where(low, qc, zero) if half == 0 else jnp.where(low, zero, qc)
                s = lax.dot_general(qm, kk, (((1,), (1,)), ((), ())), preferred_element_type=jnp.float32)
                if local:
                    s = jnp.concatenate([jnp.where(band, s[:, :3 * BLOCK], NEG), s[:, 3 * BLOCK:]], axis=1)
                sink = sink_ref[head]
                m = jnp.maximum(jnp.max(s, axis=1, keepdims=True), sink)
                p = jnp.exp(s - m)
                den = jnp.sum(p, axis=1, keepdims=True) + jnp.exp(sink - m)
                p = (p * (1.0 / den)).astype(jnp.bfloat16)
                acc = acc + _dot(p, v_halves[half])
            o_ref[0, :, chunk] = acc.astype(o_ref.dtype)


def _attention(q, k, v, kx, vx, sink, *, local):
    nb, seq, _ = q.shape
    nblk = seq // BLOCK
    cx = kx.shape[1]
    q_spec = pl.BlockSpec((1, BLOCK, ATTN_WIDTH), lambda bi, n: (bi, n, 0))
    x_spec = pl.BlockSpec((1, cx, KV_DUP), lambda bi, n: (bi, 0, 0))
    sink_spec = pl.BlockSpec(memory_space=pltpu.SMEM)
    if local:
        prev = pl.BlockSpec((1, BLOCK, KV_DUP), lambda bi, n: (bi, jnp.maximum(n - 1, 0), 0))
        cur = pl.BlockSpec((1, BLOCK, KV_DUP), lambda bi, n: (bi, n, 0))
        nxt = pl.BlockSpec((1, BLOCK, KV_DUP), lambda bi, n: (bi, jnp.minimum(n + 1, nblk - 1), 0))
        in_specs = [sink_spec, q_spec, prev, cur, nxt, prev, cur, nxt, x_spec, x_spec]
        args = (sink, q, k, k, k, v, v, v, kx, vx)
    else:
        in_specs = [sink_spec, q_spec, x_spec, x_spec]
        args = (sink, q, kx, vx)
    return pl.pallas_call(
        functools.partial(_attn_kernel, local=local),
        grid=(nb, nblk),
        in_specs=in_specs,
        out_specs=pl.BlockSpec((1, BLOCK, ATTN_WIDTH), lambda bi, n: (bi, n, 0)),
        out_shape=jax.ShapeDtypeStruct((nb, seq, ATTN_WIDTH), jnp.bfloat16),
        compiler_params=_params(2),
        name="attn_local" if local else "attn_ctx",
    )(*args)


def _mix_kernel(x_ref, mod_ref, pool_ref, pool_prev_ref, pool_next_ref, conv_ref, conv_prev_ref, conv_next_ref,
                attn_ref, wg_ref, bg_ref, band_ref, pool_bd_ref, pool_scale_ref, w_pool_ref, w_attn_ref,
                conv_w_ref, conv_b_ref, conv_g_ref, conv_beta_ref, w_conv_ref, w_out_ref, ln_g_ref, ln_b_ref,
                o_ref, glu_ref, *, seq, alpha):
    tm = x_ref.shape[1]
    d = x_ref.shape[2]
    j = pl.program_id(1)
    has_prev = j > 0
    has_next = j < pl.num_programs(1) - 1
    x = x_ref[0]
    mod = mod_ref[0]
    hb = (_ln(x) * (1.0 + mod[1:2]) + mod[0:1]).astype(jnp.bfloat16)

    zero = jnp.zeros((), jnp.bfloat16)
    pool_ext = jnp.concatenate([jnp.where(has_prev, pool_prev_ref[0], zero), pool_ref[0],
                                jnp.where(has_next, pool_next_ref[0], zero)], axis=0)
    lane_group = lax.broadcasted_iota(jnp.int32, (1, POOL_WIDTH), 1) // POOL_GROUP
    pooled_chunks = []
    for r in range(tm // POOL_CHUNK):
        ext = pool_ext[r * POOL_CHUNK:r * POOL_CHUNK + POOL_CHUNK + 2 * HALO]
        t = j * tm + r * POOL_CHUNK + lax.broadcasted_iota(jnp.int32, (POOL_CHUNK, 1), 0)
        pooled = jnp.zeros((POOL_CHUNK, POOL_WIDTH), jnp.float32)
        for gi, w in enumerate(POOL_WINDOWS):
            cnt = jnp.clip(t + w - w // 2, 0, seq) - jnp.clip(t - w // 2, 0, seq)
            part = _dot(band_ref[gi], ext) * (1.0 / cnt.astype(jnp.float32))
            pooled = jnp.where(lane_group == gi, part, pooled)
        pooled_chunks.append(pooled)
    pooled = jnp.concatenate(pooled_chunks, axis=0)
    dpool = (pooled - pool_ref[0].astype(jnp.float32)).astype(jnp.bfloat16)
    y_mix = (_dot(dpool, pool_bd_ref[...]) * pool_scale_ref[...]).astype(jnp.bfloat16)
    y_pool = _dot(y_mix, w_pool_ref[...])

    def glu(u):
        u = u.astype(jnp.float32)
        return u[:, :CONV_WIDTH] * jax.nn.sigmoid(u[:, CONV_WIDTH:])

    glu_ref[0:HALO, :] = jnp.where(has_prev, glu(conv_prev_ref[0]), 0.0)
    glu_ref[HALO:HALO + tm, :] = glu(conv_ref[0])
    glu_ref[HALO + tm:, :] = jnp.where(has_next, glu(conv_next_ref[0]), 0.0)
    rows = 64
    conv_chunks = []
    for r in range(tm // rows):
        acc = jnp.zeros((rows, CONV_WIDTH), jnp.float32)
        for tap in range(CONV_K):
            start = r * rows + HALO - CONV_K // 2 + tap
            acc = acc + glu_ref[start:start + rows, :] * conv_w_ref[tap:tap + 1, :]
        conv_chunks.append(acc)
    yc = jnp.concatenate(conv_chunks, axis=0) + conv_b_ref[...]
    yc = _ln(yc) * conv_g_ref[...] + conv_beta_ref[...]
    yc = (yc * jax.nn.sigmoid(yc)).astype(jnp.bfloat16)
    y_conv = _dot(yc, w_conv_ref[...])

    y_attn = _dot(attn_ref[0], w_attn_ref[...])

    def gate(i):
        return jax.nn.sigmoid(_dot(hb, wg_ref[:, i * d:(i + 1) * d]) + bg_ref[:, i * d:(i + 1) * d])

    merged = gate(0) * y_pool + gate(1) * y_attn + gate(2) * y_conv
    y = _dot(merged.astype(jnp.bfloat16), w_out_ref[...])
    o_ref[0] = _ln(alpha * x + mod[2:3] * y) * ln_g_ref[...] + ln_b_ref[...]


def _mix(x, mod, pool, conv, attn, wts, *, tm, mod_per_batch, alpha):
    nb, seq, d = x.shape
    nj = seq // tm
    per_tile = tm // HALO
    n_halo = seq // HALO
    mod_map = (lambda bi, j: (bi, 0, 0)) if mod_per_batch else (lambda bi, j: (0, 0, 0))

    def tile(width):
        return pl.BlockSpec((1, tm, width), lambda bi, j: (bi, j, 0))

    def prev(width):
        return pl.BlockSpec((1, HALO, width), lambda bi, j: (bi, jnp.maximum(j * per_tile - 1, 0), 0))

    def nxt(width):
        return pl.BlockSpec((1, HALO, width), lambda bi, j: (bi, jnp.minimum((j + 1) * per_tile, n_halo - 1), 0))

    in_specs = [tile(d), pl.BlockSpec((1, 6, d), mod_map),
                tile(POOL_WIDTH), prev(POOL_WIDTH), nxt(POOL_WIDTH),
                tile(2 * CONV_WIDTH), prev(2 * CONV_WIDTH), nxt(2 * CONV_WIDTH),
                tile(ATTN_WIDTH)] + [_const_spec(w.shape) for w in wts]
    return pl.pallas_call(
        functools.partial(_mix_kernel, seq=seq, alpha=alpha),
        grid=(nb, nj),
        in_specs=in_specs,
        out_specs=tile(d),
        out_shape=jax.ShapeDtypeStruct((nb, seq, d), jnp.float32),
        scratch_shapes=[pltpu.VMEM((tm + 2 * HALO, CONV_WIDTH), jnp.float32)],
        compiler_params=_params(2),
        name="mix",
    )(x, mod, pool, pool, pool, conv, conv, conv, attn, *wts)


def _mlp_kernel(x_ref, mod_ref, w1_ref, b1_ref, w2_ref, b2_ref, ln_g_ref, ln_b_ref, o_ref, *, alpha, ff_chunk):
    x = x_ref[0]
    mod = mod_ref[0]
    hb = (_ln(x) * (1.0 + mod[4:5]) + mod[3:4]).astype(jnp.bfloat16)
    d_ff = w1_ref.shape[1]
    acc = jnp.zeros(x.shape, jnp.float32)
    for c in range(d_ff // ff_chunk):
        sl = slice(c * ff_chunk, (c + 1) * ff_chunk)
        a = jnp.maximum(_dot(hb, w1_ref[:, sl]) + b1_ref[:, sl], 0.0)
        acc = acc + _dot((a * a).astype(jnp.bfloat16), w2_ref[sl, :])
    y = acc + b2_ref[...]
    o_ref[0] = _ln(alpha * x + mod[5:6] * y) * ln_g_ref[...] + ln_b_ref[...]


def _mlp(x, mod, wts, *, tm, mod_per_batch, alpha):
    nb, seq, d = x.shape
    mod_map = (lambda bi, j: (bi, 0, 0)) if mod_per_batch else (lambda bi, j: (0, 0, 0))
    tile = pl.BlockSpec((1, tm, d), lambda bi, j: (bi, j, 0))
    return pl.pallas_call(
        functools.partial(_mlp_kernel, alpha=alpha, ff_chunk=1024),
        grid=(nb, seq // tm),
        in_specs=[tile, pl.BlockSpec((1, 6, d), mod_map)] + [_const_spec(w.shape) for w in wts],
        out_specs=tile,
        out_shape=jax.ShapeDtypeStruct((nb, seq, d), jnp.float32),
        compiler_params=_params(2),
        name="mlp",
    )(x, mod, *wts)


def _rope_tables(seq):
    quarter = HEAD_DIM // 4
    inv = ROPE_BASE ** (-jnp.arange(quarter, dtype=jnp.float32) / quarter)
    t = jnp.arange(seq, dtype=jnp.int32)
    lane = np.arange(LANES)
    dim = lane % HEAD_DIM
    use_col = jnp.asarray((dim >= HEAD_DIM // 2)[None, :])
    pos = jnp.where(use_col, (t % GRID_W)[:, None], (t // GRID_W)[:, None]).astype(jnp.float32)
    ang = pos * inv[jnp.asarray(dim % quarter)][None, :]
    sign = jnp.asarray(np.where(dim % (2 * quarter) < quarter, -1.0, 1.0).astype(np.float32))[None, :]
    return jnp.cos(ang), jnp.sin(ang) * sign


def _pool_bands():
    t = np.arange(POOL_CHUNK)[:, None]
    s = np.arange(POOL_CHUNK + 2 * HALO)[None, :] - HALO
    bands = [((s >= t - w // 2) & (s <= t + w - w // 2 - 1)) for w in POOL_WINDOWS]
    return jnp.asarray(np.stack(bands).astype(np.float32), dtype=jnp.bfloat16)


def _dup_heads(w):
    h0, h1 = w[..., :HEAD_DIM], w[..., HEAD_DIM:]
    return jnp.concatenate([h0, h0, h1, h1], axis=-1)


def kernel(x, c, ctx, c_ctx, w_mod, b_mod, w_in, b_in, pool_w, pool_scale, w_pool_out, attn_sink, w_attn_out,
           conv_w, conv_b, conv_ln_g, conv_ln_b, w_conv_out, w_out, ln1_g, ln1_b, w_mlp1, b_mlp1, w_mlp2, b_mlp2,
           ln2_g, ln2_b):
    nb, seq, d = x.shape
    n_ctx = ctx.shape[1]
    depth = w_mod.shape[0]
    alpha = (2 * depth) ** 0.25
    bf = jnp.bfloat16
    assert seq % 512 == 0 and n_ctx % 256 == 0 and d % LANES == 0

    o_q = POOL_WIDTH
    o_k = o_q + ATTN_WIDTH
    o_v = o_k + N_KV_HEADS * HEAD_DIM
    o_conv = o_v + N_KV_HEADS * HEAD_DIM
    o_gate = o_conv + 2 * CONV_WIDTH

    cvec = jnp.zeros((8, d), jnp.float32).at[:nb].set(c).at[nb].set(c_ctx)
    mods = _modulation(cvec, w_mod, b_mod).reshape(depth, 8, 6, d)
    rope = _rope_tables(seq)
    bands = _pool_bands()

    xl, xc = x, ctx
    for i in range(depth):
        last = i == depth - 1
        mod_l = mods[i, :nb]
        mod_c = mods[i, nb:nb + 1]

        def relayout(w):
            return jnp.concatenate([w[..., :o_k], _dup_heads(w[..., o_k:o_v]), _dup_heads(w[..., o_v:o_conv]),
                                    w[..., o_conv:o_gate]], axis=-1)

        w_proj = relayout(w_in[i]).astype(bf)
        b_proj = relayout(b_in[i])[None, :]
        pool_bd = jnp.zeros((POOL_WIDTH, POOL_WIDTH), jnp.float32)
        for gi in range(len(POOL_WINDOWS)):
            gs = slice(gi * POOL_GROUP, (gi + 1) * POOL_GROUP)
            pool_bd = pool_bd.at[gs, gs].set(pool_w[i, gi])
        mix_w = (w_in[i][:, o_gate:].astype(bf), b_in[i][None, o_gate:], bands, pool_bd.astype(bf),
                 pool_scale[i][None, :], w_pool_out[i].astype(bf), w_attn_out[i].astype(bf),
                 conv_w[i], conv_b[i][None, :], conv_ln_g[i][None, :], conv_ln_b[i][None, :],
                 w_conv_out[i].astype(bf), w_out[i].astype(bf), ln1_g[i][None, :], ln1_b[i][None, :])
        mlp_w = (w_mlp1[i].astype(bf), b_mlp1[i][None, :], w_mlp2[i].astype(bf), b_mlp2[i][None, :],
                 ln2_g[i][None, :], ln2_b[i][None, :])

        pool_l, q_l, k_l, v_l, conv_l = _inproj(xl, mod_l, w_proj, b_proj, rope, tm=512, mod_per_batch=True)
        pool_c, q_c, k_c, v_c, conv_c = _inproj(xc, mod_c, w_proj, b_proj, None, tm=256, mod_per_batch=False)
        o_l = _attention(q_l, k_l, v_l, k_c, v_c, attn_sink[i], local=True)
        xl = _mix(xl, mod_l, pool_l, conv_l, o_l, mix_w, tm=256, mod_per_batch=True, alpha=alpha)
        xl = _mlp(xl, mod_l, mlp_w, tm=512, mod_per_batch=True, alpha=alpha)
        if not last:
            o_c = _attention(q_c, None, None, k_c, v_c, attn_sink[i], local=False)
            xc = _mix(xc, mod_c, pool_c, conv_c, o_c, mix_w, tm=256, mod_per_batch=False, alpha=alpha)
            xc = _mlp(xc, mod_c, mlp_w, tm=256, mod_per_batch=False, alpha=alpha)
    return xl
```

```python
import functools
import math

import jax
import jax.numpy as jnp
import numpy as np
from jax import lax
from jax.experimental import pallas as pl
from jax.experimental.pallas import tpu as pltpu

LANES = 128
BF16_SUBLANES = 16
VMEM_LIMIT_BYTES = 56 * 1024 * 1024

GRID_W = 64
HEAD_DIM = 64
N_Q_HEADS = 8
N_KV_HEADS = 2
ATTN_WIDTH = N_Q_HEADS * HEAD_DIM
WINDOW = 128
BLOCK = 128
ROPE_BASE = 10000.0
POOL_WINDOWS = (2, 4, 8, 16)
POOL_GROUP = 64
POOL_WIDTH = POOL_GROUP * len(POOL_WINDOWS)
CONV_WIDTH = 256
CONV_K = 31
LN_EPS = 1e-5
NEG = -1e30
LOG2_E = math.log2(math.e)

HALO = BF16_SUBLANES
POOL_CHUNK = 128
KV_DUP = 2 * N_KV_HEADS * HEAD_DIM

SEG_POOL = (0, POOL_WIDTH)
SEG_Q = (SEG_POOL[1], SEG_POOL[1] + ATTN_WIDTH)
SEG_K = (SEG_Q[1], SEG_Q[1] + KV_DUP)
SEG_V = (SEG_K[1], SEG_K[1] + KV_DUP)
SEG_CONV = (SEG_V[1], SEG_V[1] + 2 * CONV_WIDTH)
PROJ_WIDTH = SEG_CONV[1]


def _ln(x):
    mu = jnp.mean(x, axis=-1, keepdims=True)
    xc = x - mu
    var = jnp.mean(xc * xc, axis=-1, keepdims=True)
    return xc * lax.rsqrt(var + LN_EPS)


def _dot(a, b):
    return jnp.dot(a, b, preferred_element_type=jnp.float32)


def _const_spec(shape):
    nd = len(shape)
    return pl.BlockSpec(shape, lambda *_: (0,) * nd, pipeline_mode=pl.Buffered(1))


def _params(n_axes):
    return pltpu.CompilerParams(dimension_semantics=("arbitrary",) * n_axes,
                                vmem_limit_bytes=VMEM_LIMIT_BYTES)


def _mod_kernel(c_ref, w_ref, b_ref, o_ref):
    c = c_ref[...]
    s = (c * jax.nn.sigmoid(c)).astype(jnp.bfloat16)
    o_ref[0] = _dot(s, w_ref[0].astype(jnp.bfloat16)) + b_ref[0]


def _modulation(cvec, w_mod, b_mod):
    depth, d, n = w_mod.shape
    tn = 1536
    rows = cvec.shape[0]
    return pl.pallas_call(
        _mod_kernel,
        grid=(depth, n // tn),
        in_specs=[pl.BlockSpec((rows, d), lambda i, j: (0, 0)),
                  pl.BlockSpec((1, d, tn), lambda i, j: (i, 0, j)),
                  pl.BlockSpec((1, 1, tn), lambda i, j: (i, 0, j))],
        out_specs=pl.BlockSpec((1, rows, tn), lambda i, j: (i, 0, j)),
        out_shape=jax.ShapeDtypeStruct((depth, rows, n), jnp.float32),
        compiler_params=_params(2),
        name="modulation",
    )(cvec, w_mod, b_mod.reshape(depth, 1, n))


def _rope(xc, cos, sin_signed, first_half):
    partner = jnp.where(first_half, pltpu.roll(xc, LANES - 16, axis=1), pltpu.roll(xc, 16, axis=1))
    return xc * cos + partner * sin_signed


def _inproj_kernel(*refs, use_rope):
    if use_rope:
        x_ref, mod_ref, w_ref, b_ref, cos_ref, sin_ref, pool_ref, q_ref, k_ref, v_ref, conv_ref = refs
    else:
        x_ref, mod_ref, w_ref, b_ref, pool_ref, q_ref, k_ref, v_ref, conv_ref = refs
    x = x_ref[0]
    mod = mod_ref[0]
    h = _ln(x) * (1.0 + mod[1:2]) + mod[0:1]
    hb = h.astype(jnp.bfloat16)

    def proj(seg):
        return _dot(hb, w_ref[:, seg[0]:seg[1]]) + b_ref[:, seg[0]:seg[1]]

    pool_ref[0] = proj(SEG_POOL).astype(pool_ref.dtype)
    v_ref[0] = proj(SEG_V).astype(v_ref.dtype)
    conv_ref[0] = proj(SEG_CONV).astype(conv_ref.dtype)

    q = proj(SEG_Q)
    k = proj(SEG_K)
    if use_rope:
        cos = cos_ref[...]
        sin = sin_ref[...]
        lane = lax.broadcasted_iota(jnp.int32, (1, LANES), 1)
        first_half = (lane % 32) < 16
        q = jnp.concatenate([_rope(q[:, c * LANES:(c + 1) * LANES], cos, sin, first_half)
                             for c in range(ATTN_WIDTH // LANES)], axis=1)
        k = jnp.concatenate([_rope(k[:, c * LANES:(c + 1) * LANES], cos, sin, first_half)
                             for c in range(KV_DUP // LANES)], axis=1)
    q_ref[0] = (q * (LOG2_E * HEAD_DIM ** -0.5)).astype(q_ref.dtype)
    k_ref[0] = k.astype(k_ref.dtype)


def _inproj(x, mod, w, b, rope, *, tm, mod_per_batch):
    nb, seq, d = x.shape
    use_rope = rope is not None
    mod_map = (lambda bi, j: (bi, 0, 0)) if mod_per_batch else (lambda bi, j: (0, 0, 0))
    in_specs = [pl.BlockSpec((1, tm, d), lambda bi, j: (bi, j, 0)),
                pl.BlockSpec((1, 6, d), mod_map),
                _const_spec((d, PROJ_WIDTH)),
                _const_spec((1, PROJ_WIDTH))]
    args = [x, mod, w, b]
    if use_rope:
        in_specs += [pl.BlockSpec((tm, LANES), lambda bi, j: (j, 0))] * 2
        args += list(rope)
    widths = (POOL_WIDTH, ATTN_WIDTH, KV_DUP, KV_DUP, 2 * CONV_WIDTH)
    return pl.pallas_call(
        functools.partial(_inproj_kernel, use_rope=use_rope),
        grid=(nb, seq // tm),
        in_specs=in_specs,
        out_specs=[pl.BlockSpec((1, tm, wd), lambda bi, j: (bi, j, 0)) for wd in widths],
        out_shape=[jax.ShapeDtypeStruct((nb, seq, wd), jnp.bfloat16) for wd in widths],
        compiler_params=_params(2),
        name="inproj_rope" if use_rope else "inproj",
    )(*args)


def _attn_kernel(*refs, local):
    if local:
        (sink_ref, q_ref, kp_ref, kc_ref, kn_ref, vp_ref, vc_ref, vn_ref, kx_ref, vx_ref, o_ref) = refs
    else:
        sink_ref, q_ref, kx_ref, vx_ref, o_ref = refs
    tq = q_ref.shape[1]
    lane = lax.broadcasted_iota(jnp.int32, (1, LANES), 1)
    low = lane < HEAD_DIM
    zero = jnp.zeros((), jnp.bfloat16)
    heads_per_kv = N_Q_HEADS // N_KV_HEADS
    if local:
        n = pl.program_id(1)
        n_sub = tq // BLOCK
        rows = BLOCK
        row = lax.broadcasted_iota(jnp.int32, (BLOCK, BLOCK), 0)
        col = lax.broadcasted_iota(jnp.int32, (BLOCK, BLOCK), 1)
        upper = jnp.where(col >= row, 0.0, NEG)
        lower = jnp.where(col <= row, 0.0, NEG)
        has_prev = n > 0
        has_next = n < pl.num_programs(1) - 1
    else:
        n_sub = 1
        rows = tq
    for u in range(n_sub):
        for g in range(N_KV_HEADS):
            sl = slice(g * LANES, (g + 1) * LANES)
            if local:
                def blocks(prev_ref, cur_ref, next_ref, ctx_ref):
                    local_blocks = ([prev_ref[0, :, sl]]
                                    + [cur_ref[0, i * BLOCK:(i + 1) * BLOCK, sl] for i in range(n_sub)]
                                    + [next_ref[0, :, sl]])
                    return jnp.concatenate(local_blocks[u:u + 3] + [ctx_ref[0, :, sl]], axis=0)
                kk = blocks(kp_ref, kc_ref, kn_ref, kx_ref)
                vv = blocks(vp_ref, vc_ref, vn_ref, vx_ref)
                bias_first = jnp.where(has_prev, upper, NEG) if u == 0 else upper
                bias_last = jnp.where(has_next, lower, NEG) if u == n_sub - 1 else lower
            else:
                kk = kx_ref[0, :, sl]
                vv = vx_ref[0, :, sl]
            q_rows = []
            for c in range(LANES // HEAD_DIM):
                qc = q_ref[0, u * rows:(u + 1) * rows, (2 * g + c) * LANES:(2 * g + c + 1) * LANES]
                q_rows += [jnp.where(low, qc, zero), jnp.where(low, zero, qc)]
            s_all = lax.dot_general(jnp.concatenate(q_rows, axis=0), kk, (((1,), (1,)), ((), ())),
                                    preferred_element_type=jnp.float32)
            p_rows, inv_den = [], []
            for h in range(heads_per_kv):
                s = s_all[h * rows:(h + 1) * rows]
                if local:
                    s = jnp.concatenate([s[:, :BLOCK] + bias_first, s[:, BLOCK:2 * BLOCK],
                                         s[:, 2 * BLOCK:3 * BLOCK] + bias_last, s[:, 3 * BLOCK:]], axis=1)
                sink = sink_ref[g * heads_per_kv + h] * LOG2_E
                m = jnp.maximum(jnp.max(s, axis=1, keepdims=True), sink)
                p = jnp.exp2(s - m)
                inv_den.append(1.0 / (jnp.sum(p, axis=1, keepdims=True) + jnp.exp2(sink - m)))
                p_rows.append(p.astype(jnp.bfloat16))
            o_all = _dot(jnp.concatenate(p_rows, axis=0), vv) * jnp.concatenate(inv_den, axis=0)
            for c in range(LANES // HEAD_DIM):
                o_pair = jnp.where(low, o_all[2 * c * rows:(2 * c + 1) * rows],
                                   o_all[(2 * c + 1) * rows:(2 * c + 2) * rows])
                o_ref[0, u * rows:(u + 1) * rows, (2 * g + c) * LANES:(2 * g + c + 1) * LANES] = (
                    o_pair.astype(o_ref.dtype))


def _attention(q, k, v, kx, vx, sink, *, local, tq):
    nb, seq, _ = q.shape
    per_tile = tq // BLOCK
    nblk = seq // BLOCK
    cx = kx.shape[1]
    q_spec = pl.BlockSpec((1, tq, ATTN_WIDTH), lambda bi, n: (bi, n, 0))
    x_spec = pl.BlockSpec((1, cx, KV_DUP), lambda bi, n: (bi, 0, 0))
    sink_spec = pl.BlockSpec(memory_space=pltpu.SMEM)
    if local:
        prev = pl.BlockSpec((1, BLOCK, KV_DUP), lambda bi, n: (bi, jnp.maximum(n * per_tile - 1, 0), 0))
        cur = pl.BlockSpec((1, tq, KV_DUP), lambda bi, n: (bi, n, 0))
        nxt = pl.BlockSpec((1, BLOCK, KV_DUP), lambda bi, n: (bi, jnp.minimum((n + 1) * per_tile, nblk - 1), 0))
        in_specs = [sink_spec, q_spec, prev, cur, nxt, prev, cur, nxt, x_spec, x_spec]
        args = (sink, q, k, k, k, v, v, v, kx, vx)
    else:
        in_specs = [sink_spec, q_spec, x_spec, x_spec]
        args = (sink, q, kx, vx)
    return pl.pallas_call(
        functools.partial(_attn_kernel, local=local),
        grid=(nb, seq // tq),
        in_specs=in_specs,
        out_specs=pl.BlockSpec((1, tq, ATTN_WIDTH), lambda bi, n: (bi, n, 0)),
        out_shape=jax.ShapeDtypeStruct((nb, seq, ATTN_WIDTH), jnp.bfloat16),
        compiler_params=_params(2),
        name="attn_local" if local else "attn_ctx",
    )(*args)


def _mix_kernel(x_ref, mod_ref, pool_ref, pool_prev_ref, pool_next_ref, conv_ref, conv_prev_ref, conv_next_ref,
                attn_ref, wg_ref, bg_ref, band_ref, pool_bd_ref, pool_scale_ref, w_pool_ref, w_attn_ref,
                conv_w_ref, conv_b_ref, conv_g_ref, conv_beta_ref, w_conv_ref, w_out_ref, ln_g_ref, ln_b_ref,
                o_ref, glu_ref, *, seq, alpha):
    tm = x_ref.shape[1]
    d = x_ref.shape[2]
    j = pl.program_id(1)
    has_prev = j > 0
    has_next = j < pl.num_programs(1) - 1
    x = x_ref[0]
    mod = mod_ref[0]
    hb = (_ln(x) * (1.0 + mod[1:2]) + mod[0:1]).astype(jnp.bfloat16)

    zero = jnp.zeros((), jnp.bfloat16)
    pool_ext = jnp.concatenate([jnp.where(has_prev, pool_prev_ref[0], zero), pool_ref[0],
                                jnp.where(has_next, pool_next_ref[0], zero)], axis=0)
    lane_group = lax.broadcasted_iota(jnp.int32, (1, POOL_WIDTH), 1) // POOL_GROUP
    pooled_chunks = []
    for r in range(tm // POOL_CHUNK):
        ext = pool_ext[r * POOL_CHUNK:r * POOL_CHUNK + POOL_CHUNK + 2 * HALO]
        t = j * tm + r * POOL_CHUNK + lax.broadcasted_iota(jnp.int32, (POOL_CHUNK, 1), 0)
        pooled = jnp.zeros((POOL_CHUNK, POOL_WIDTH), jnp.float32)
        for gi, w in enumerate(POOL_WINDOWS):
            cnt = jnp.clip(t + w - w // 2, 0, seq) - jnp.clip(t - w // 2, 0, seq)
            part = _dot(band_ref[gi], ext) * (1.0 / cnt.astype(jnp.float32))
            pooled = jnp.where(lane_group == gi, part, pooled)
        pooled_chunks.append(pooled)
    pooled = jnp.concatenate(pooled_chunks, axis=0)
    dpool = (pooled - pool_ref[0].astype(jnp.float32)).astype(jnp.bfloat16)
    y_mix = (_dot(dpool, pool_bd_ref[...]) * pool_scale_ref[...]).astype(jnp.bfloat16)
    y_pool = _dot(y_mix, w_pool_ref[...])

    def glu(u):
        u = u.astype(jnp.float32)
        return u[:, :CONV_WIDTH] * jax.nn.sigmoid(u[:, CONV_WIDTH:])

    glu_ref[0:HALO, :] = jnp.where(has_prev, glu(conv_prev_ref[0]), 0.0)
    glu_ref[HALO:HALO + tm, :] = glu(conv_ref[0])
    glu_ref[HALO + tm:, :] = jnp.where(has_next, glu(conv_next_ref[0]), 0.0)
    rows = 64
    conv_chunks = []
    for r in range(tm // rows):
        acc = jnp.zeros((rows, CONV_WIDTH), jnp.float32)
        for tap in range(CONV_K):
            start = r * rows + HALO - CONV_K // 2 + tap
            acc = acc + glu_ref[start:start + rows, :] * conv_w_ref[tap:tap + 1, :]
        conv_chunks.append(acc)
    yc = jnp.concatenate(conv_chunks, axis=0) + conv_b_ref[...]
    yc = _ln(yc) * conv_g_ref[...] + conv_beta_ref[...]
    yc = (yc * jax.nn.sigmoid(yc)).astype(jnp.bfloat16)
    y_conv = _dot(yc, w_conv_ref[...])

    y_attn = _dot(attn_ref[0], w_attn_ref[...])

    def gate(i):
        return jax.nn.sigmoid(_dot(hb, wg_ref[:, i * d:(i + 1) * d]) + bg_ref[:, i * d:(i + 1) * d])

    merged = gate(0) * y_pool + gate(1) * y_attn + gate(2) * y_conv
    y = _dot(merged.astype(jnp.bfloat16), w_out_ref[...])
    o_ref[0] = _ln(alpha * x + mod[2:3] * y) * ln_g_ref[...] + ln_b_ref[...]


def _mix(x, mod, pool, conv, attn, wts, *, tm, mod_per_batch, alpha):
    nb, seq, d = x.shape
    nj = seq // tm
    per_tile = tm // HALO
    n_halo = seq // HALO
    mod_map = (lambda bi, j: (bi, 0, 0)) if mod_per_batch else (lambda bi, j: (0, 0, 0))

    def tile(width):
        return pl.BlockSpec((1, tm, width), lambda bi, j: (bi, j, 0))

    def prev(width):
        return pl.BlockSpec((1, HALO, width), lambda bi, j: (bi, jnp.maximum(j * per_tile - 1, 0), 0))

    def nxt(width):
        return pl.BlockSpec((1, HALO, width), lambda bi, j: (bi, jnp.minimum((j + 1) * per_tile, n_halo - 1), 0))

    in_specs = [tile(d), pl.BlockSpec((1, 6, d), mod_map),
                tile(POOL_WIDTH), prev(POOL_WIDTH), nxt(POOL_WIDTH),
                tile(2 * CONV_WIDTH), prev(2 * CONV_WIDTH), nxt(2 * CONV_WIDTH),
                tile(ATTN_WIDTH)] + [_const_spec(w.shape) for w in wts]
    return pl.pallas_call(
        functools.partial(_mix_kernel, seq=seq, alpha=alpha),
        grid=(nb, nj),
        in_specs=in_specs,
        out_specs=tile(d),
        out_shape=jax.ShapeDtypeStruct((nb, seq, d), jnp.float32),
        scratch_shapes=[pltpu.VMEM((tm + 2 * HALO, CONV_WIDTH), jnp.float32)],
        compiler_params=_params(2),
        name="mix",
    )(x, mod, pool, pool, pool, conv, conv, conv, attn, *wts)


def _mlp_kernel(x_ref, mod_ref, w1_ref, b1_ref, w2_ref, b2_ref, ln_g_ref, ln_b_ref, o_ref, *, alpha, ff_chunk):
    x = x_ref[0]
    mod = mod_ref[0]
    hb = (_ln(x) * (1.0 + mod[4:5]) + mod[3:4]).astype(jnp.bfloat16)
    d_ff = w1_ref.shape[1]
    acc = jnp.zeros(x.shape, jnp.float32)
    for c in range(d_ff // ff_chunk):
        sl = slice(c * ff_chunk, (c + 1) * ff_chunk)
        a = jnp.maximum(_dot(hb, w1_ref[:, sl]) + b1_ref[:, sl], 0.0)
        acc = acc + _dot((a * a).astype(jnp.bfloat16), w2_ref[sl, :])
    y = acc + b2_ref[...]
    o_ref[0] = _ln(alpha * x + mod[5:6] * y) * ln_g_ref[...] + ln_b_ref[...]


def _mlp(x, mod, wts, *, tm, mod_per_batch, alpha):
    nb, seq, d = x.shape
    mod_map = (lambda bi, j: (bi, 0, 0)) if mod_per_batch else (lambda bi, j: (0, 0, 0))
    tile = pl.BlockSpec((1, tm, d), lambda bi, j: (bi, j, 0))
    return pl.pallas_call(
        functools.partial(_mlp_kernel, alpha=alpha, ff_chunk=1024),
        grid=(nb, seq // tm),
        in_specs=[tile, pl.BlockSpec((1, 6, d), mod_map)] + [_const_spec(w.shape) for w in wts],
        out_specs=tile,
        out_shape=jax.ShapeDtypeStruct((nb, seq, d), jnp.float32),
        compiler_params=_params(2),
        name="mlp",
    )(x, mod, *wts)


def _rope_tables(seq):
    quarter = HEAD_DIM // 4
    inv = ROPE_BASE ** (-jnp.arange(quarter, dtype=jnp.float32) / quarter)
    t = jnp.arange(seq, dtype=jnp.int32)
    lane = np.arange(LANES)
    dim = lane % HEAD_DIM
    use_col = jnp.asarray((dim >= HEAD_DIM // 2)[None, :])
    pos = jnp.where(use_col, (t % GRID_W)[:, None], (t // GRID_W)[:, None]).astype(jnp.float32)
    ang = pos * inv[jnp.asarray(dim % quarter)][None, :]
    sign = jnp.asarray(np.where(dim % (2 * quarter) < quarter, -1.0, 1.0).astype(np.float32))[None, :]
    return jnp.cos(ang), jnp.sin(ang) * sign


def _pool_bands():
    t = np.arange(POOL_CHUNK)[:, None]
    s = np.arange(POOL_CHUNK + 2 * HALO)[None, :] - HALO
    bands = [((s >= t - w // 2) & (s <= t + w - w // 2 - 1)) for w in POOL_WINDOWS]
    return jnp.asarray(np.stack(bands).astype(np.float32), dtype=jnp.bfloat16)


def _dup_heads(w):
    h0, h1 = w[..., :HEAD_DIM], w[..., HEAD_DIM:]
    return jnp.concatenate([h0, h0, h1, h1], axis=-1)


def kernel(x, c, ctx, c_ctx, w_mod, b_mod, w_in, b_in, pool_w, pool_scale, w_pool_out, attn_sink, w_attn_out,
           conv_w, conv_b, conv_ln_g, conv_ln_b, w_conv_out, w_out, ln1_g, ln1_b, w_mlp1, b_mlp1, w_mlp2, b_mlp2,
           ln2_g, ln2_b):
    nb, seq, d = x.shape
    n_ctx = ctx.shape[1]
    depth = w_mod.shape[0]
    alpha = (2 * depth) ** 0.25
    bf = jnp.bfloat16
    assert seq % 512 == 0 and n_ctx % 256 == 0 and d % LANES == 0

    o_q = POOL_WIDTH
    o_k = o_q + ATTN_WIDTH
    o_v = o_k + N_KV_HEADS * HEAD_DIM
    o_conv = o_v + N_KV_HEADS * HEAD_DIM
    o_gate = o_conv + 2 * CONV_WIDTH

    cvec = jnp.zeros((8, d), jnp.float32).at[:nb].set(c).at[nb].set(c_ctx)
    mods = _modulation(cvec, w_mod, b_mod).reshape(depth, 8, 6, d)
    rope = _rope_tables(seq)
    bands = _pool_bands()

    xl, xc = x, ctx
    for i in range(depth):
        last = i == depth - 1
        mod_l = mods[i, :nb]
        mod_c = mods[i, nb:nb + 1]

        def relayout(w):
            return jnp.concatenate([w[..., :o_k], _dup_heads(w[..., o_k:o_v]), _dup_heads(w[..., o_v:o_conv]),
                                    w[..., o_conv:o_gate]], axis=-1)

        w_proj = relayout(w_in[i]).astype(bf)
        b_proj = relayout(b_in[i])[None, :]
        pool_bd = jnp.zeros((POOL_WIDTH, POOL_WIDTH), jnp.float32)
        for gi in range(len(POOL_WINDOWS)):
            gs = slice(gi * POOL_GROUP, (gi + 1) * POOL_GROUP)
            pool_bd = pool_bd.at[gs, gs].set(pool_w[i, gi])
        mix_w = (w_in[i][:, o_gate:].astype(bf), b_in[i][None, o_gate:], bands, pool_bd.astype(bf),
                 pool_scale[i][None, :], w_pool_out[i].astype(bf), w_attn_out[i].astype(bf),
                 conv_w[i], conv_b[i][None, :], conv_ln_g[i][None, :], conv_ln_b[i][None, :],
                 w_conv_out[i].astype(bf), w_out[i].astype(bf), ln1_g[i][None, :], ln1_b[i][None, :])
        mlp_w = (w_mlp1[i].astype(bf), b_mlp1[i][None, :], w_mlp2[i].astype(bf), b_mlp2[i][None, :],
                 ln2_g[i][None, :], ln2_b[i][None, :])

        pool_l, q_l, k_l, v_l, conv_l = _inproj(xl, mod_l, w_proj, b_proj, rope, tm=512, mod_per_batch=True)
        pool_c, q_c, k_c, v_c, conv_c = _inproj(xc, mod_c, w_proj, b_proj, None, tm=256, mod_per_batch=False)
        o_l = _attention(q_l, k_l, v_l, k_c, v_c, attn_sink[i], local=True, tq=256)
        xl = _mix(xl, mod_l, pool_l, conv_l, o_l, mix_w, tm=256, mod_per_batch=True, alpha=alpha)
        xl = _mlp(xl, mod_l, mlp_w, tm=512, mod_per_batch=True, alpha=alpha)
        if not last:
            o_c = _attention(q_c, None, None, k_c, v_c, attn_sink[i], local=False, tq=128)
            xc = _mix(xc, mod_c, pool_c, conv_c, o_c, mix_w, tm=256, mod_per_batch=False, alpha=alpha)
            xc = _mlp(xc, mod_c, mlp_w, tm=256, mod_per_batch=False, alpha=alpha)
    return xl
```
